```python
import jax, jax.numpy as jnp
from jax import lax
import numpy as np

D_MODEL = 1024
BATCH = 8
SEQ = 4096
DEPTH = 1
DEC_BATCH = 32
DEC_SEQ = 1
PAST_LEN = 16384
PAGE_SIZE = 128

RET_HEADS = 4
RET_DK = 128
RET_DV = 256
FOX_HEADS = 8
FOX_HD = 64
RET_QK_W = RET_HEADS * RET_DK
RET_V_W = RET_HEADS * RET_DV
FOX_W = FOX_HEADS * FOX_HD
D_FF = 2816
CONV_W = 3
CHUNK = 128
Q_BLOCK = 128
ROPE_BASE = 10000.0
LN_EPS = 1e-5
GN_EPS = 1e-5
ALPHA = (2 * DEPTH) ** 0.25
BETA = (8 * DEPTH) ** -0.25
IN_SPLITS = (RET_QK_W, RET_QK_W, RET_V_W, RET_V_W, FOX_W, FOX_W, FOX_W, FOX_HEADS, D_MODEL, D_MODEL)
IN_COLS = sum(IN_SPLITS)

kernel_name = 'hybrid_retention_fox_convffn_step'


def layer_norm(x, w, b):
    x32 = x.astype(jnp.float32)
    mu = jnp.mean(x32, -1, keepdims=True)
    var = jnp.mean(jnp.square(x32 - mu), -1, keepdims=True)
    return ((x32 - mu) * lax.rsqrt(var + LN_EPS) * w + b).astype(x.dtype)


def rotary(x, pos):
    half = x.shape[-1] // 2
    freq = ROPE_BASE ** (-jnp.arange(half, dtype=jnp.float32) / half)
    ang = pos[:, None] * freq[None, :]
    cos = jnp.cos(ang)[None, :, None, :]
    sin = jnp.sin(ang)[None, :, None, :]
    x32 = x.astype(jnp.float32)
    x1, x2 = x32[..., :half], x32[..., half:]
    return jnp.concatenate([x1 * cos - x2 * sin, x1 * sin + x2 * cos], -1).astype(x.dtype)


def in_projection(x, w_in, b_f, pos):
    b, t, _ = x.shape
    offsets = np.cumsum(IN_SPLITS)[:-1].tolist()
    rq, rk, rv, rg, fq, fk, fv, ff, ga, gb = jnp.split(x @ w_in, offsets, axis=-1)
    rq = rotary(rq.reshape(b, t, RET_HEADS, RET_DK), pos)
    rk = rotary(rk.reshape(b, t, RET_HEADS, RET_DK), pos) * (RET_DK ** -0.5)
    rv = rv.reshape(b, t, RET_HEADS, RET_DV)
    fq = fq.reshape(b, t, FOX_HEADS, FOX_HD)
    fk = fk.reshape(b, t, FOX_HEADS, FOX_HD)
    fv = fv.reshape(b, t, FOX_HEADS, FOX_HD)
    logf = jax.nn.log_sigmoid(ff.astype(jnp.float32) + b_f.astype(jnp.float32))
    return rq, rk, rv, rg, fq, fk, fv, logf, ga, gb


def retention_chunkwise(q, k, v, s0):
    q = q.astype(jnp.float32)
    k = k.astype(jnp.float32)
    v = v.astype(jnp.float32)
    b, t, h, _ = q.shape
    dv = v.shape[-1]
    c = CHUNK if t % CHUNK == 0 else t
    n = t // c
    log_g = jnp.log(1.0 - 2.0 ** (-5.0 - jnp.arange(h, dtype=jnp.float32)))
    i = jnp.arange(c, dtype=jnp.float32)
    rel = i[:, None] - i[None, :]
    intra = jnp.where(rel >= 0, jnp.exp(log_g[:, None, None] * jnp.maximum(rel, 0.0)), 0.0)
    q_decay = jnp.exp(log_g[None, :] * (i[:, None] + 1.0))
    k_decay = jnp.exp(log_g[None, :] * (c - 1.0 - i[:, None]))
    chunk_decay = jnp.exp(log_g * c)

    def to_chunks(a):
        return a.reshape(b, n, c, *a.shape[2:]).swapaxes(0, 1)

    def step(s, qkv):
        qc, kc, vc = qkv
        inner = jnp.einsum('bihd,bjhd->bhij', qc, kc) * intra
        o = jnp.einsum('bhij,bjhe->bihe', inner, vc) + jnp.einsum('bihd,bhde->bihe', qc * q_decay[None, :, :, None], s)
        s = s * chunk_decay[None, :, None, None] + jnp.einsum('bjhd,bjhe->bhde', kc * k_decay[None, :, :, None], vc)
        return s, o

    s, o = lax.scan(step, s0.astype(jnp.float32), (to_chunks(q), to_chunks(k), to_chunks(v)))
    return o.swapaxes(0, 1).reshape(b, t, h, dv), s


def fox_prompt(q, k, v, logf):
    b, t, h, d = q.shape
    scale = d ** -0.5
    c = jnp.cumsum(logf, axis=1)
    nb = t // Q_BLOCK
    qb = q.reshape(b, nb, Q_BLOCK, h, d).swapaxes(0, 1)
    cb = c.reshape(b, nb, Q_BLOCK, h).swapaxes(0, 1)
    starts = jnp.arange(nb, dtype=jnp.int32) * Q_BLOCK
    kpos = jnp.arange(t, dtype=jnp.int32)
    c_k = c.transpose(0, 2, 1)[:, :, None, :]

    def block(args):
        qi, ci, start = args
        s = jnp.einsum('bqhd,bkhd->bhqk', qi, k).astype(jnp.float32) * scale
        bias = ci.transpose(0, 2, 1)[..., None] - c_k
        qpos = start + jnp.arange(Q_BLOCK, dtype=jnp.int32)
        logits = jnp.where(kpos[None, :] <= qpos[:, None], s + bias, -jnp.inf)
        p = jax.nn.softmax(logits, axis=-1)
        return jnp.einsum('bhqk,bkhd->bqhd', p.astype(v.dtype), v)

    o = lax.map(block, (qb, cb, starts))
    return o.swapaxes(0, 1).reshape(b, t, h, d)


def fox_sample(q, k, v, logf, k_past, v_past, logf_past):
    p_len = k_past.shape[1]
    t = q.shape[1]
    scale = q.shape[-1] ** -0.5
    c = jnp.cumsum(jnp.concatenate([logf_past.astype(jnp.float32), logf], axis=1), axis=1)
    s = jnp.concatenate([jnp.einsum('bqhd,bkhd->bhqk', q, k_past.astype(q.dtype)),
                         jnp.einsum('bqhd,bkhd->bhqk', q, k)], axis=-1).astype(jnp.float32) * scale
    bias = c[:, p_len:].transpose(0, 2, 1)[..., None] - c.transpose(0, 2, 1)[:, :, None, :]
    kpos = jnp.arange(p_len + t, dtype=jnp.int32)
    qpos = p_len + jnp.arange(t, dtype=jnp.int32)
    logits = jnp.where(kpos[None, :] <= qpos[:, None], s + bias, -jnp.inf)
    p = jax.nn.softmax(logits, axis=-1).astype(v.dtype)
    return (jnp.einsum('bhqk,bkhd->bqhd', p[..., :p_len], v_past.astype(v.dtype))
            + jnp.einsum('bhqk,bkhd->bqhd', p[..., p_len:], v))


def merge_branches(ret_o, rg, fox_o, ga, gb, gn_w, gn_b, w_pa, w_pb, w_o):
    b, t = rg.shape[:2]
    mu = jnp.mean(ret_o, -1, keepdims=True)
    var = jnp.mean(jnp.square(ret_o - mu), -1, keepdims=True)
    rn = ((ret_o - mu) * lax.rsqrt(var + GN_EPS)).reshape(b, t, RET_V_W)
    rn = (rn * gn_w + gn_b).astype(rg.dtype)
    u_a = (jax.nn.silu(rg) * rn) @ w_pa
    u_b = fox_o.reshape(b, t, FOX_W) @ w_pb
    merged = jax.nn.sigmoid(ga) * u_a + jax.nn.sigmoid(gb) * u_b
    return merged @ w_o


def conv_ffn(h, conv_prev, w_up, w_gate, conv_w, conv_b, w_down):
    u = h @ w_up
    g = h @ w_gate
    t = u.shape[1]
    up = jnp.concatenate([conv_prev.astype(u.dtype), u], axis=1)
    acc = conv_b
    for i in range(CONV_W):
        acc = acc + up[:, i:i + t] * conv_w[i]
    a = jax.nn.gelu(acc) * g
    return a @ w_down, up[:, up.shape[1] - (CONV_W - 1):]


def post_norm_ffn(x, mix, conv_prev, ln1_w, ln1_b, w_up, w_gate, conv_w, conv_b, w_down, ln2_w, ln2_b):
    h = layer_norm(ALPHA * x + mix, ln1_w, ln1_b)
    f, conv_new = conv_ffn(h, conv_prev, w_up, w_gate, conv_w, conv_b, w_down)
    return layer_norm(ALPHA * h + f, ln2_w, ln2_b), conv_new


def setup_inputs(seed: int = 0) -> dict:
    key = jax.random.key(seed)
    ks = jax.random.split(key, 32)
    f32 = jnp.float32
    n_pages = PAST_LEN // PAGE_SIZE
    n_pool = (DEC_BATCH * n_pages * 5) // 4
    nrm = lambda k, shape: jax.random.normal(k, shape, f32)
    page_table = jax.random.permutation(ks[0], n_pool)[:DEC_BATCH * n_pages].reshape(DEC_BATCH, n_pages).astype(jnp.int32)
    scales = (1.0, 1.0, BETA, 1.0, 1.0, 1.0, BETA, 1.0, 1.0, 1.0)
    col_scale = jnp.concatenate([jnp.full((n,), s, f32) for n, s in zip(IN_SPLITS, scales)])
    return {
        'x_prompt': nrm(ks[1], (BATCH, SEQ, D_MODEL)),
        'x_sample': nrm(ks[2], (DEC_BATCH, DEC_SEQ, D_MODEL)),
        'cache_k': nrm(ks[3], (DEPTH, n_pool, PAGE_SIZE, FOX_HEADS, FOX_HD)),
        'cache_v': nrm(ks[4], (DEPTH, n_pool, PAGE_SIZE, FOX_HEADS, FOX_HD)),
        'cache_logf': jax.nn.log_sigmoid(2.0 + nrm(ks[5], (DEPTH, n_pool, PAGE_SIZE, FOX_HEADS))),
        'state_ret': 0.5 * nrm(ks[6], (DEPTH, DEC_BATCH, RET_HEADS, RET_DK, RET_DV)),
        'state_conv': nrm(ks[7], (DEPTH, DEC_BATCH, CONV_W - 1, D_FF)),
        'page_table': page_table,
        'w_in': nrm(ks[8], (DEPTH, D_MODEL, IN_COLS)) * (D_MODEL ** -0.5) * col_scale,
        'b_f': 2.0 + 0.5 * nrm(ks[9], (DEPTH, FOX_HEADS)),
        'ret_gn_w': 1.0 + 0.02 * nrm(ks[10], (DEPTH, RET_V_W)),
        'ret_gn_b': 0.02 * nrm(ks[11], (DEPTH, RET_V_W)),
        'w_pa': nrm(ks[12], (DEPTH, RET_V_W, D_MODEL)) * (RET_V_W ** -0.5) * BETA,
        'w_pb': nrm(ks[13], (DEPTH, FOX_W, D_MODEL)) * (FOX_W ** -0.5) * BETA,
        'w_o': nrm(ks[14], (DEPTH, D_MODEL, D_MODEL)) * (D_MODEL ** -0.5) * BETA,
        'ln1_w': 1.0 + 0.02 * nrm(ks[15], (DEPTH, D_MODEL)),
        'ln1_b': 0.02 * nrm(ks[16], (DEPTH, D_MODEL)),
        'w_up': nrm(ks[17], (DEPTH, D_MODEL, D_FF)) * (D_MODEL ** -0.5) * BETA,
        'w_gate': nrm(ks[18], (DEPTH, D_MODEL, D_FF)) * (D_MODEL ** -0.5) * BETA,
        'conv_w': nrm(ks[19], (DEPTH, CONV_W, D_FF)) * (CONV_W ** -0.5),
        'conv_b': 0.02 * nrm(ks[20], (DEPTH, D_FF)),
        'w_down': nrm(ks[21], (DEPTH, D_FF, D_MODEL)) * (D_FF ** -0.5) * BETA,
        'ln2_w': 1.0 + 0.02 * nrm(ks[22], (DEPTH, D_MODEL)),
        'ln2_b': 0.02 * nrm(ks[23], (DEPTH, D_MODEL)),
    }


def reference(x_prompt, x_sample, cache_k, cache_v, cache_logf, state_ret, state_conv, page_table,
              w_in, b_f, ret_gn_w, ret_gn_b, w_pa, w_pb, w_o, ln1_w, ln1_b,
              w_up, w_gate, conv_w, conv_b, w_down, ln2_w, ln2_b):
    xp, xs = x_prompt, x_sample
    bp, tp = xp.shape[0], xp.shape[1]
    bs, ts = xs.shape[0], xs.shape[1]
    n_pages = page_table.shape[1]
    past = n_pages * PAGE_SIZE
    pos_p = jnp.arange(tp, dtype=jnp.float32)
    pos_s = past + jnp.arange(ts, dtype=jnp.float32)
    kp_l, vp_l, lfp_l, rp_l, cp_l = [], [], [], [], []
    ks_l, vs_l, lfs_l, rs_l, cs_l = [], [], [], [], []
    for l in range(DEPTH):
        mix_w = (ret_gn_w[l], ret_gn_b[l], w_pa[l], w_pb[l], w_o[l])
        ffn_w = (ln1_w[l], ln1_b[l], w_up[l], w_gate[l], conv_w[l], conv_b[l], w_down[l], ln2_w[l], ln2_b[l])
        rq, rk, rv, rg, fq, fk, fv, lf, ga, gb = in_projection(xp, w_in[l], b_f[l], pos_p)
        ret_o, s_p = retention_chunkwise(rq, rk, rv, jnp.zeros((bp, RET_HEADS, RET_DK, RET_DV), jnp.float32))
        fox_o = fox_prompt(fq, fk, fv, lf)
        mix = merge_branches(ret_o, rg, fox_o, ga, gb, *mix_w)
        xp, c_p = post_norm_ffn(xp, mix, jnp.zeros((bp, CONV_W - 1, D_FF), xp.dtype), *ffn_w)
        kp_l.append(fk); vp_l.append(fv); lfp_l.append(lf); rp_l.append(s_p); cp_l.append(c_p)
        rq, rk, rv, rg, fq, fk, fv, lf, ga, gb = in_projection(xs, w_in[l], b_f[l], pos_s)
        ret_o, s_s = retention_chunkwise(rq, rk, rv, state_ret[l])
        k_past = cache_k[l][page_table].reshape(bs, past, FOX_HEADS, FOX_HD)
        v_past = cache_v[l][page_table].reshape(bs, past, FOX_HEADS, FOX_HD)
        lf_past = cache_logf[l][page_table].reshape(bs, past, FOX_HEADS)
        fox_o = fox_sample(fq, fk, fv, lf, k_past, v_past, lf_past)
        mix = merge_branches(ret_o, rg, fox_o, ga, gb, *mix_w)
        xs, c_s = post_norm_ffn(xs, mix, state_conv[l], *ffn_w)
        ks_l.append(fk); vs_l.append(fv); lfs_l.append(lf); rs_l.append(s_s); cs_l.append(c_s)
    k_prompt = jnp.stack(kp_l)
    v_prompt = jnp.stack(vp_l)
    logf_prompt = jnp.stack(lfp_l)
    k_sample = jnp.stack(ks_l)
    v_sample = jnp.stack(vs_l)
    logf_sample = jnp.stack(lfs_l)
    ret_prompt = jnp.stack(rp_l)
    ret_sample = jnp.stack(rs_l)
    conv_prompt = jnp.stack(cp_l)
    conv_sample = jnp.stack(cs_l)
    return (xp, xs, k_prompt, v_prompt, logf_prompt, k_sample, v_sample, logf_sample,
            ret_prompt, ret_sample, conv_prompt, conv_sample)
```

```python
import functools

import jax
import jax.numpy as jnp
from jax import lax
from jax.experimental import pallas as pl
from jax.experimental.pallas import tpu as pltpu

F32 = jnp.float32
BF16 = jnp.bfloat16

D_MODEL = 1024
RET_HEADS = 4
RET_DK = 128
RET_DV = 256
FOX_HEADS = 8
FOX_HD = 64
RET_QK_W = RET_HEADS * RET_DK
RET_V_W = RET_HEADS * RET_DV
FOX_W = FOX_HEADS * FOX_HD
D_FF = 2816
CONV_W = 3
CHUNK = 128
PAGE_SIZE = 128
ROPE_BASE = 10000.0
LN_EPS = 1e-5
GN_EPS = 1e-5
DEPTH = 1
ALPHA = (2 * DEPTH) ** 0.25
IN_SPLITS = (RET_QK_W, RET_QK_W, RET_V_W, RET_V_W, FOX_W, FOX_W, FOX_W, FOX_HEADS, D_MODEL, D_MODEL)
FOX_SCALE = FOX_HD ** -0.5

LANES = 128
V7X_VMEM_LIMIT_BYTES = 56 * 1024 * 1024
NEG_BIG = -1e30

IN_PROJ_TM = 256
ATTN_TQ = 512
ATTN_TK = 512
MERGE_TM = 512
FFN_TM = 256
PAGES_PER_STEP = 8


def _dot(a, b):
    return jnp.dot(a, b, preferred_element_type=F32)


def _dot_nt(a, b):
    return lax.dot_general(a, b, (((1,), (1,)), ((), ())), preferred_element_type=F32)


def _const_spec(shape):
    nd = len(shape)
    return pl.BlockSpec(shape, lambda *_: (0,) * nd, pipeline_mode=pl.Buffered(1))


def _params(*semantics):
    return pltpu.CompilerParams(dimension_semantics=semantics, vmem_limit_bytes=V7X_VMEM_LIMIT_BYTES)


def _layer_norm(t, w, b):
    mu = jnp.mean(t, axis=-1, keepdims=True)
    d = t - mu
    var = jnp.mean(d * d, axis=-1, keepdims=True)
    return d * lax.rsqrt(var + LN_EPS) * w + b


def _log_sigmoid(x):
    return jnp.minimum(x, 0.0) - jnp.log1p(jnp.exp(-jnp.abs(x)))


def _in_proj_kernel(x_ref, cos_ref, sin_ref, wrq, wrk, wrv, wrg, wfq, wfk, wfv, wff, wga, wgb, bf_ref,
                    rq_o, rk_o, rv_o, rg_o, fq_o, fk_o, fv_o, fkb_o, fvb_o, lf_o, ga_o, gb_o):
    xb = x_ref[...].astype(BF16)
    cos = cos_ref[...]
    sin = sin_ref[...]

    def rotary_to(o_ref, y, scale):
        for h in range(RET_HEADS):
            yh = y[:, h * RET_DK:(h + 1) * RET_DK]
            r = yh * cos + pltpu.roll(yh, RET_DK // 2, 1) * sin
            if scale is not None:
                r = r * scale
            o_ref[:, h * RET_DK:(h + 1) * RET_DK] = r

    rotary_to(rq_o, _dot(xb, wrq[...]), None)
    rotary_to(rk_o, _dot(xb, wrk[...]), RET_DK ** -0.5)
    rv_o[...] = _dot(xb, wrv[...]).astype(BF16)
    rg_o[...] = _dot(xb, wrg[...])
    fq_o[...] = (_dot(xb, wfq[...]) * FOX_SCALE).astype(BF16)
    fk = _dot(xb, wfk[...])
    fk_o[...] = fk
    fkb_o[...] = fk.astype(BF16)
    fv = _dot(xb, wfv[...])
    fv_o[...] = fv
    fvb_o[...] = fv.astype(BF16)
    ff = _dot(xb, wff[...])[:, :FOX_HEADS] + bf_ref[...]
    lf_o[...] = _log_sigmoid(ff)
    ga_o[...] = _dot(xb, wga[...])
    gb_o[...] = _dot(xb, wgb[...])


def _in_proj(x, cos_t, sin_t, ws, b_f, tm):
    n = x.shape[0]
    n_tab = cos_t.shape[0] // tm
    row = lambda w: pl.BlockSpec((tm, w), lambda i: (i, 0))
    tab = pl.BlockSpec((tm, RET_DK), lambda i: (i % n_tab, 0))
    out_widths = (RET_QK_W, RET_QK_W, RET_V_W, RET_V_W, FOX_W, FOX_W, FOX_W, FOX_W, FOX_W, FOX_HEADS,
                  D_MODEL, D_MODEL)
    out_dtypes = (F32, F32, BF16, F32, BF16, F32, F32, BF16, BF16, F32, F32, F32)
    return pl.pallas_call(
        _in_proj_kernel,
        grid=(n // tm,),
        in_specs=[row(D_MODEL), tab, tab] + [_const_spec(w.shape) for w in ws] + [_const_spec(b_f.shape)],
        out_specs=[row(w) for w in out_widths],
        out_shape=[jax.ShapeDtypeStruct((n, w), d) for w, d in zip(out_widths, out_dtypes)],
        compiler_params=_params("arbitrary"),
        name="in_proj",
    )(x, cos_t, sin_t, *ws, b_f)


def _prefix_lanes(x):
    n = x.shape[-1]
    lane = lax.broadcasted_iota(jnp.int32, x.shape, x.ndim - 1)
    s = 1
    while s < n:
        x = x + jnp.where(lane >= s, pltpu.roll(x, s, x.ndim - 1), 0.0)
        s *= 2
    return x


def _cumsum_kernel(x_ref, o_ref):
    o_ref[...] = _prefix_lanes(x_ref[...])


def _cumsum_rows(x):
    return pl.pallas_call(
        _cumsum_kernel,
        out_shape=jax.ShapeDtypeStruct(x.shape, x.dtype),
        name="logf_cumsum",
    )(x)


def _group_norm_gate(o, g, gnw, gnb):
    mu = jnp.mean(o, axis=-1, keepdims=True)
    d = o - mu
    var = jnp.mean(d * d, axis=-1, keepdims=True)
    rn = d * lax.rsqrt(var + GN_EPS) * gnw + gnb
    return (g * jax.nn.sigmoid(g)) * rn


def _retention_kernel(rq_ref, rk_ref, rv_ref, rg_ref, intra_ref, qd_ref, kd_ref, cd_ref, gnw_ref, gnb_ref,
                      a_ref, s_ref):
    j = pl.program_id(1)

    @pl.when(j == 0)
    def _():
        s_ref[...] = jnp.zeros_like(s_ref)

    for h in range(RET_HEADS):
        qs = slice(h * RET_DK, (h + 1) * RET_DK)
        vs = slice(h * RET_DV, (h + 1) * RET_DV)
        q = rq_ref[:, qs]
        k = rk_ref[:, qs]
        v = rv_ref[:, vs]
        s = s_ref[0, h]
        inner = _dot_nt(q.astype(BF16), k.astype(BF16)) * intra_ref[h]
        o = _dot(inner.astype(BF16), v) + _dot((q * qd_ref[h]).astype(BF16), s.astype(BF16))
        kt = (k * kd_ref[h]).T
        s_ref[0, h] = s * cd_ref[h] + _dot(kt.astype(BF16), v)
        a_ref[:, vs] = _group_norm_gate(o, rg_ref[:, vs], gnw_ref[:, vs], gnb_ref[:, vs]).astype(BF16)


def _retention_prompt(rq, rk, rv, rg, tabs, gnw, gnb, batch, seq):
    intra, qd, kd, cd = tabs
    c = intra.shape[-1]
    nc = seq // c
    row = lambda w: pl.BlockSpec((c, w), lambda b, j: (b * nc + j, 0))
    return pl.pallas_call(
        _retention_kernel,
        grid=(batch, nc),
        in_specs=[row(RET_QK_W), row(RET_QK_W), row(RET_V_W), row(RET_V_W),
                  _const_spec(intra.shape), _const_spec(qd.shape), _const_spec(kd.shape), _const_spec(cd.shape),
                  _const_spec(gnw.shape), _const_spec(gnb.shape)],
        out_specs=[row(RET_V_W),
                   pl.BlockSpec((1, RET_HEADS, RET_DK, RET_DV), lambda b, j: (b, 0, 0, 0))],
        out_shape=[jax.ShapeDtypeStruct((batch * seq, RET_V_W), BF16),
                   jax.ShapeDtypeStruct((batch, RET_HEADS, RET_DK, RET_DV), F32)],
        compiler_params=_params("arbitrary", "arbitrary"),
        name="retention_prompt",
    )(rq, rk, rv, rg, intra, qd, kd, cd, gnw, gnb)


def _attn_kernel(q_ref, k_ref, v_ref, crow_ref, ccol_ref, o_ref, *, tq, tk):
    i = pl.program_id(2)
    q = q_ref[...].astype(F32)
    lane = lax.broadcasted_iota(jnp.int32, (1, LANES), 1)
    row_id = lax.broadcasted_iota(jnp.int32, (tq, tk), 0)
    col_id = lax.broadcasted_iota(jnp.int32, (tq, tk), 1)
    outs = []
    for hh in range(2):
        sel = (lane < FOX_HD) if hh == 0 else (lane >= FOX_HD)
        qh = jnp.where(sel, q, 0.0).astype(BF16)
        ct = ccol_ref[0, 0, :, hh:hh + 1]

        def step(kt, carry, diagonal):
            m, l, acc = carry
            k0 = pl.multiple_of(kt * tk, tk)
            kb = k_ref[pl.ds(k0, tk), :]
            vb = v_ref[pl.ds(k0, tk), :]
            z = _dot_nt(qh, kb) - crow_ref[0, 0, hh:hh + 1, pl.ds(k0, tk)]
            if diagonal:
                z = jnp.where(col_id <= row_id, z, NEG_BIG)
            m_new = jnp.maximum(m, jnp.max(z, axis=1, keepdims=True) + ct)
            alpha = jnp.exp(m - m_new)
            p = jnp.exp(z + (ct - m_new))
            l = alpha * l + jnp.sum(p, axis=1, keepdims=True)
            acc = alpha * acc + _dot(p.astype(BF16), vb)
            return m_new, l, acc

        carry = (jnp.full((tq, 1), NEG_BIG, F32), jnp.zeros((tq, 1), F32), jnp.zeros((tq, LANES), F32))
        carry = lax.fori_loop(0, i, functools.partial(step, diagonal=False), carry)
        m, l, acc = step(i, carry, True)
        outs.append(acc / l)
    o_ref[...] = jnp.where(lane < FOX_HD, outs[0], outs[1]).astype(BF16)


def _attention_prompt(fq, fkb, fvb, c_row, c_col, batch, seq):
    tq, tk = ATTN_TQ, ATTN_TK
    assert tq == tk and seq % tq == 0
    nq = seq // tq
    pairs = FOX_HEADS // 2
    kv = pl.BlockSpec((seq, LANES), lambda b, p, i: (b, p))
    return pl.pallas_call(
        functools.partial(_attn_kernel, tq=tq, tk=tk),
        grid=(batch, pairs, nq),
        in_specs=[pl.BlockSpec((tq, LANES), lambda b, p, i: (b * nq + i, p)), kv, kv,
                  pl.BlockSpec((1, 1, 2, seq), lambda b, p, i: (b, p, 0, 0)),
                  pl.BlockSpec((1, 1, tq, 2), lambda b, p, i: (b, p, i, 0))],
        out_specs=pl.BlockSpec((tq, LANES), lambda b, p, i: (b * nq + i, p)),
        out_shape=jax.ShapeDtypeStruct((batch * seq, FOX_W), BF16),
        compiler_params=_params("arbitrary", "arbitrary", "arbitrary"),
        name="fox_attention_prompt",
    )(fq, fkb, fvb, c_row, c_col)


def _merge_kernel(a_ref, fo_ref, ga_ref, gb_ref, x_ref, wpa, wpb, wo, lnw, lnb, h_ref):
    ua = _dot(a_ref[...], wpa[...])
    ub = _dot(fo_ref[...], wpb[...])
    merged = jax.nn.sigmoid(ga_ref[...]) * ua + jax.nn.sigmoid(gb_ref[...]) * ub
    mix = _dot(merged.astype(BF16), wo[...])
    h_ref[...] = _layer_norm(ALPHA * x_ref[...] + mix, lnw[...], lnb[...])


def _merge(a, fo, ga, gb, x, wpa, wpb, wo, lnw, lnb, tm):
    n = x.shape[0]
    row = lambda w: pl.BlockSpec((tm, w), lambda i: (i, 0))
    return pl.pallas_call(
        _merge_kernel,
        grid=(n // tm,),
        in_specs=[row(RET_V_W), row(FOX_W), row(D_MODEL), row(D_MODEL), row(D_MODEL)]
                 + [_const_spec(w.shape) for w in (wpa, wpb, wo, lnw, lnb)],
        out_specs=row(D_MODEL),
        out_shape=jax.ShapeDtypeStruct((n, D_MODEL), F32),
        compiler_params=_params("arbitrary"),
        name="merge_ln1",
    )(a, fo, ga, gb, x, wpa, wpb, wo, lnw, lnb)


def _ffn_tail(h, u, u1, u2, g, cw, cb, wdown, lnw, lnb):
    acc = cb + u2 * cw[0:1] + u1 * cw[1:2] + u * cw[2:3]
    a = jax.nn.gelu(acc) * g
    f = _dot(a.astype(BF16), wdown)
    return _layer_norm(ALPHA * h + f, lnw, lnb)


def _ffn_prompt_kernel(h_ref, wup, wgate, cw_ref, cb_ref, wdown, lnw, lnb, y_ref, conv_ref, prev_ref, *, tm, tps):
    i = pl.program_id(0)
    h = h_ref[...]
    hb = h.astype(BF16)
    u = _dot(hb, wup[...])
    g = _dot(hb, wgate[...])
    @pl.when(i % tps == 0)
    def _():
        prev_ref[...] = jnp.zeros_like(prev_ref)

    prev = prev_ref[...]
    rid = lax.broadcasted_iota(jnp.int32, (tm, 1), 0)
    u1 = jnp.where(rid == 0, prev[7:8], pltpu.roll(u, 1, 0))
    u2 = jnp.where(rid == 0, prev[6:7], jnp.where(rid == 1, prev[7:8], pltpu.roll(u, 2, 0)))
    prev_ref[...] = u[tm - 8:tm]
    conv_ref[0] = u[tm - (CONV_W - 1):tm]
    y_ref[...] = _ffn_tail(h, u, u1, u2, g, cw_ref[...], cb_ref[...], wdown[...], lnw[...], lnb[...])


def _ffn_prompt(h, wup, wgate, cw, cb, wdown, lnw, lnb, batch, seq, tm):
    n = h.shape[0]
    tps = seq // tm
    row = pl.BlockSpec((tm, D_MODEL), lambda i: (i, 0))
    return pl.pallas_call(
        functools.partial(_ffn_prompt_kernel, tm=tm, tps=tps),
        grid=(n // tm,),
        in_specs=[row] + [_const_spec(w.shape) for w in (wup, wgate, cw, cb, wdown, lnw, lnb)],
        out_specs=[row, pl.BlockSpec((1, CONV_W - 1, D_FF), lambda i: (i // tps, 0, 0))],
        out_shape=[jax.ShapeDtypeStruct((n, D_MODEL), F32),
                   jax.ShapeDtypeStruct((batch, CONV_W - 1, D_FF), F32)],
        scratch_shapes=[pltpu.VMEM((8, D_FF), F32)],
        compiler_params=_params("arbitrary"),
        name="conv_ffn_prompt",
    )(h, wup, wgate, cw, cb, wdown, lnw, lnb)


def _ffn_sample_kernel(h_ref, sc0_ref, sc1_ref, wup, wgate, cw_ref, cb_ref, wdown, lnw, lnb, y_ref, u_ref):
    h = h_ref[...]
    hb = h.astype(BF16)
    u = _dot(hb, wup[...])
    g = _dot(hb, wgate[...])
    u_ref[...] = u
    y_ref[...] = _ffn_tail(h, u, sc1_ref[...], sc0_ref[...], g, cw_ref[...], cb_ref[...], wdown[...],
                           lnw[...], lnb[...])


def _ffn_sample(h, sc0, sc1, wup, wgate, cw, cb, wdown, lnw, lnb):
    n = h.shape[0]
    return pl.pallas_call(
        _ffn_sample_kernel,
        out_shape=[jax.ShapeDtypeStruct((n, D_MODEL), F32), jax.ShapeDtypeStruct((n, D_FF), F32)],
        compiler_params=pltpu.CompilerParams(vmem_limit_bytes=V7X_VMEM_LIMIT_BYTES),
        name="conv_ffn_sample",
    )(h, sc0, sc1, wup, wgate, cw, cb, wdown, lnw, lnb)


def _row_to_col(row, eye):
    n = row.shape[-1]
    return jnp.sum(jnp.where(eye, jnp.broadcast_to(row, (n, n)), 0.0), axis=1, keepdims=True)


def _retention_sample_kernel(rq_ref, rk_ref, rv_ref, rg_ref, s_ref, intra_ref, qd_ref, kd_ref, cd_ref,
                             gnw_ref, gnb_ref, a_ref, so_ref):
    eye = (lax.broadcasted_iota(jnp.int32, (RET_DK, RET_DK), 0)
           == lax.broadcasted_iota(jnp.int32, (RET_DK, RET_DK), 1))
    for h in range(RET_HEADS):
        qs = slice(h * RET_DK, (h + 1) * RET_DK)
        vs = slice(h * RET_DV, (h + 1) * RET_DV)
        q = rq_ref[0, :, qs]
        k = rk_ref[0, :, qs]
        v = rv_ref[0, :, vs].astype(F32)
        s = s_ref[0, h]
        inner = jnp.sum(q * k, axis=1, keepdims=True) * intra_ref[h]
        qcol = _row_to_col(q * qd_ref[h], eye)
        kcol = _row_to_col(k * kd_ref[h], eye)
        o = inner * v + jnp.sum(qcol * s, axis=0, keepdims=True)
        so_ref[0, h] = s * cd_ref[h] + kcol * v
        a_ref[0, :, vs] = _group_norm_gate(o, rg_ref[0, :, vs], gnw_ref[:, vs], gnb_ref[:, vs]).astype(BF16)


def _retention_sample(rq, rk, rv, rg, state, tabs, gnw, gnb):
    intra, qd, kd, cd = tabs
    nb = rq.shape[0]
    row = lambda w: pl.BlockSpec((1, 1, w), lambda b: (b, 0, 0))
    st = pl.BlockSpec((1, RET_HEADS, RET_DK, RET_DV), lambda b: (b, 0, 0, 0))
    r3 = lambda t: t.reshape(nb, 1, t.shape[-1])
    a, s_new = pl.pallas_call(
        _retention_sample_kernel,
        grid=(nb,),
        in_specs=[row(RET_QK_W), row(RET_QK_W), row(RET_V_W), row(RET_V_W), st]
                 + [_const_spec(t.shape) for t in (intra, qd, kd, cd, gnw, gnb)],
        out_specs=[row(RET_V_W), st],
        out_shape=[jax.ShapeDtypeStruct((nb, 1, RET_V_W), BF16),
                   jax.ShapeDtypeStruct(state.shape, F32)],
        compiler_params=_params("arbitrary"),
        name="retention_sample",
    )(r3(rq), r3(rk), r3(rv), r3(rg), state, intra, qd, kd, cd, gnw, gnb)
    return a.reshape(nb, RET_V_W), s_new


def _decode_attn_kernel(pt_ref, q_ref, kn_ref, vn_ref, lfn_ref, *refs, g_pages):
    del pt_ref
    k_refs = refs[:g_pages]
    v_refs = refs[g_pages:2 * g_pages]
    lf_refs = refs[2 * g_pages:3 * g_pages]
    o_ref, m_ref, l_ref, c_ref, acc_ref = refs[3 * g_pages:]
    j = pl.program_id(1)

    @pl.when(j == 0)
    def _():
        m_ref[...] = jnp.full_like(m_ref, NEG_BIG)
        l_ref[...] = jnp.zeros_like(l_ref)
        c_ref[...] = jnp.zeros_like(c_ref)
        acc_ref[...] = jnp.zeros_like(acc_ref)

    head_of_lane = lax.broadcasted_iota(jnp.int32, (FOX_HEADS, FOX_W), 1) // FOX_HD
    own = head_of_lane == lax.broadcasted_iota(jnp.int32, (FOX_HEADS, FOX_W), 0)
    qbd = jnp.where(own, jnp.broadcast_to(q_ref[0].astype(F32), (FOX_HEADS, FOX_W)), 0.0).astype(BF16)

    def update(z, pv_fn):
        m = m_ref[...]
        m_new = jnp.maximum(m, jnp.max(z, axis=1, keepdims=True))
        alpha = jnp.exp(m - m_new)
        p = jnp.exp(z - m_new)
        l_ref[...] = alpha * l_ref[...] + jnp.sum(p, axis=1, keepdims=True)
        acc_ref[...] = alpha * acc_ref[...] + pv_fn(p)
        m_ref[...] = m_new

    for g in range(g_pages):
        kb = k_refs[g][0].astype(BF16)
        vb = v_refs[g][0].astype(BF16)
        c = c_ref[...] + _prefix_lanes(lf_refs[g][0])
        c_ref[...] = c[:, PAGE_SIZE - 1:PAGE_SIZE]
        update(_dot_nt(qbd, kb) - c, lambda p: _dot(p.astype(BF16), vb))

    @pl.when(j == pl.num_programs(1) - 1)
    def _():
        kn = kn_ref[0].astype(BF16).astype(F32)
        vn = vn_ref[0].astype(BF16).astype(F32)
        z_self = jnp.sum(qbd.astype(F32) * kn, axis=1, keepdims=True) - (c_ref[...] + lfn_ref[0])
        update(z_self, lambda p: p.astype(BF16).astype(F32) * vn)
        o = jnp.where(own, acc_ref[...] / l_ref[...], 0.0)
        o_ref[0] = jnp.sum(o, axis=0, keepdims=True).astype(BF16)


def _attention_sample(page_table, fq, k_new, v_new, lf_new, cache_k, cache_v, cache_lft):
    nb, n_pages = page_table.shape
    gp = PAGES_PER_STEP
    assert n_pages % gp == 0
    row = lambda w: pl.BlockSpec((1, 1, w), lambda b, j, pt: (b, 0, 0))

    def page(g, shape):
        return pl.BlockSpec((1,) + shape, lambda b, j, pt: (pt[b, j * gp + g], 0, 0))

    in_specs = ([row(FOX_W), row(FOX_W), row(FOX_W), pl.BlockSpec((1, FOX_HEADS, 1), lambda b, j, pt: (b, 0, 0))]
                + [page(g, (PAGE_SIZE, FOX_W)) for g in range(gp)]
                + [page(g, (PAGE_SIZE, FOX_W)) for g in range(gp)]
                + [page(g, (FOX_HEADS, PAGE_SIZE)) for g in range(gp)])
    r3 = lambda t: t.reshape(nb, 1, t.shape[-1])
    out = pl.pallas_call(
        functools.partial(_decode_attn_kernel, g_pages=gp),
        grid_spec=pltpu.PrefetchScalarGridSpec(
            num_scalar_prefetch=1,
            grid=(nb, n_pages // gp),
            in_specs=in_specs,
            out_specs=row(FOX_W),
            scratch_shapes=[pltpu.VMEM((FOX_HEADS, 1), F32), pltpu.VMEM((FOX_HEADS, 1), F32),
                            pltpu.VMEM((FOX_HEADS, 1), F32), pltpu.VMEM((FOX_HEADS, FOX_W), F32)]),
        out_shape=jax.ShapeDtypeStruct((nb, 1, FOX_W), BF16),
        compiler_params=_params("arbitrary", "arbitrary"),
        name="fox_attention_sample",
    )(page_table, r3(fq), r3(k_new), r3(v_new), lf_new.reshape(nb, FOX_HEADS, 1),
      *([cache_k] * gp), *([cache_v] * gp), *([cache_lft] * gp))
    return out.reshape(nb, FOX_W)


def _rotary_tables(pos):
    half = RET_DK // 2
    freq = ROPE_BASE ** (-jnp.arange(half, dtype=F32) / half)
    ang = pos[:, None] * freq[None, :]
    cos, sin = jnp.cos(ang), jnp.sin(ang)
    return jnp.concatenate([cos, cos], -1), jnp.concatenate([-sin, sin], -1)


def _decay_tables(c):
    log_g = jnp.log(1.0 - 2.0 ** (-5.0 - jnp.arange(RET_HEADS, dtype=F32)))
    i = jnp.arange(c, dtype=F32)
    rel = i[:, None] - i[None, :]
    intra = jnp.where(rel >= 0, jnp.exp(log_g[:, None, None] * jnp.maximum(rel, 0.0)), 0.0)
    q_decay = jnp.exp(log_g[:, None] * (i[None, :] + 1.0))
    k_decay = jnp.exp(log_g[:, None] * (c - 1.0 - i[None, :]))
    chunk_decay = jnp.exp(log_g * c)
    bc = lambda t, w: jnp.broadcast_to(t[:, :, None], (RET_HEADS, c, w))
    return intra, bc(q_decay, RET_DK), bc(k_decay, RET_DK), jnp.broadcast_to(
        chunk_decay[:, None, None], (RET_HEADS, 1, RET_DV))


def _split_w_in(w_in):
    offs = [0]
    for n in IN_SPLITS:
        offs.append(offs[-1] + n)
    parts = [w_in[:, offs[i]:offs[i + 1]].astype(BF16) for i in range(len(IN_SPLITS))]
    parts[7] = jnp.pad(parts[7], ((0, 0), (0, LANES - FOX_HEADS)))
    return parts


def kernel(x_prompt, x_sample, cache_k, cache_v, cache_logf, state_ret, state_conv, page_table,
           w_in, b_f, ret_gn_w, ret_gn_b, w_pa, w_pb, w_o, ln1_w, ln1_b,
           w_up, w_gate, conv_w, conv_b, w_down, ln2_w, ln2_b):
    assert w_in.shape[0] == DEPTH
    bp, tp, _ = x_prompt.shape
    bs, ts, _ = x_sample.shape
    assert ts == 1
    n_pool = cache_k.shape[1]
    past = page_table.shape[1] * PAGE_SIZE
    np_tok = bp * tp

    ws = _split_w_in(w_in[0])
    bf2 = b_f[0].reshape(1, FOX_HEADS)
    gnw, gnb = ret_gn_w[0].reshape(1, RET_V_W), ret_gn_b[0].reshape(1, RET_V_W)
    wpa, wpb, wo = w_pa[0].astype(BF16), w_pb[0].astype(BF16), w_o[0].astype(BF16)
    wup, wgate, wdown = w_up[0].astype(BF16), w_gate[0].astype(BF16), w_down[0].astype(BF16)
    ln1w, ln1b = ln1_w[0].reshape(1, D_MODEL), ln1_b[0].reshape(1, D_MODEL)
    ln2w, ln2b = ln2_w[0].reshape(1, D_MODEL), ln2_b[0].reshape(1, D_MODEL)
    cw, cb = conv_w[0], conv_b[0].reshape(1, D_FF)

    xp = x_prompt.reshape(np_tok, D_MODEL)
    cos_p, sin_p = _rotary_tables(jnp.arange(tp, dtype=F32))
    rq, rk, rv, rg, fq, fk, fv, fkb, fvb, lf, ga, gb = _in_proj(xp, cos_p, sin_p, ws, bf2, IN_PROJ_TM)
    a_p, ret_p = _retention_prompt(rq, rk, rv, rg, _decay_tables(CHUNK if tp % CHUNK == 0 else tp), gnw, gnb, bp, tp)
    lf_t = lf.reshape(bp, tp, FOX_HEADS).transpose(0, 2, 1)
    c_row = _cumsum_rows(lf_t.reshape(bp * FOX_HEADS, tp)).reshape(bp, FOX_HEADS // 2, 2, tp)
    c_col = c_row.transpose(0, 1, 3, 2)
    fo_p = _attention_prompt(fq, fkb, fvb, c_row, c_col, bp, tp)
    h_p = _merge(a_p, fo_p, ga, gb, xp, wpa, wpb, wo, ln1w, ln1b, MERGE_TM)
    y_p, conv_p = _ffn_prompt(h_p, wup, wgate, cw, cb, wdown, ln2w, ln2b, bp, tp, FFN_TM)

    xs = x_sample.reshape(bs, D_MODEL)
    cos_s, sin_s = _rotary_tables(jnp.full((bs,), past, dtype=F32))
    rq, rk, rv, rg, fq, fk_s, fv_s, _, _, lf_s, ga, gb = _in_proj(xs, cos_s, sin_s, ws, bf2, bs)
    a_s, ret_s = _retention_sample(rq, rk, rv, rg, state_ret[0], _decay_tables(1), gnw, gnb)
    cache_lft = cache_logf[0].transpose(0, 2, 1)
    fo_s = _attention_sample(page_table, fq, fk_s, fv_s, lf_s,
                             cache_k[0].reshape(n_pool, PAGE_SIZE, FOX_W),
                             cache_v[0].reshape(n_pool, PAGE_SIZE, FOX_W), cache_lft)
    h_s = _merge(a_s, fo_s, ga, gb, xs, wpa, wpb, wo, ln1w, ln1b, bs)
    sc0, sc1 = state_conv[0, :, 0, :], state_conv[0, :, 1, :]
    y_s, u_s = _ffn_sample(h_s, sc0, sc1, wup, wgate, cw, cb, wdown, ln2w, ln2b)

    return (y_p.reshape(bp, tp, D_MODEL), y_s.reshape(bs, ts, D_MODEL),
            fk.reshape(DEPTH, bp, tp, FOX_HEADS, FOX_HD), fv.reshape(DEPTH, bp, tp, FOX_HEADS, FOX_HD),
            lf.reshape(DEPTH, bp, tp, FOX_HEADS),
            fk_s.reshape(DEPTH, bs, ts, FOX_HEADS, FOX_HD), fv_s.reshape(DEPTH, bs, ts, FOX_HEADS, FOX_HD),
            lf_s.reshape(DEPTH, bs, ts, FOX_HEADS),
            ret_p[None], ret_s[None], conv_p[None], jnp.stack([sc1, u_s], axis=1)[None])
```

```python
import functools

import jax
import jax.numpy as jnp
from jax import lax
from jax.experimental import pallas as pl
from jax.experimental.pallas import tpu as pltpu

F32 = jnp.float32
BF16 = jnp.bfloat16

D_MODEL = 1024
RET_HEADS = 4
RET_DK = 128
RET_DV = 256
FOX_HEADS = 8
FOX_HD = 64
RET_QK_W = RET_HEADS * RET_DK
RET_V_W = RET_HEADS * RET_DV
FOX_W = FOX_HEADS * FOX_HD
D_FF = 2816
CONV_W = 3
CHUNK = 128
PAGE_SIZE = 128
ROPE_BASE = 10000.0
LN_EPS = 1e-5
GN_EPS = 1e-5
DEPTH = 1
ALPHA = (2 * DEPTH) ** 0.25
IN_SPLITS = (RET_QK_W, RET_QK_W, RET_V_W, RET_V_W, FOX_W, FOX_W, FOX_W, FOX_HEADS, D_MODEL, D_MODEL)
LOG2E = 1.4426950408889634
FOX_Q_SCALE = FOX_HD ** -0.5 * LOG2E

LANES = 128
V7X_VMEM_LIMIT_BYTES = 56 * 1024 * 1024
NEG_BIG = -1e30

IN_PROJ_TM = 256
ATTN_TQ = 512
ATTN_TK = 512
MERGE_TM = 512
FFN_TM = 512
PAGES_PER_STEP = 16


def _dot(a, b):
    return jnp.dot(a, b, preferred_element_type=F32)


def _dot_nt(a, b):
    return lax.dot_general(a, b, (((1,), (1,)), ((), ())), preferred_element_type=F32)


def _const_spec(shape):
    nd = len(shape)
    return pl.BlockSpec(shape, lambda *_: (0,) * nd, pipeline_mode=pl.Buffered(1))


def _params(*semantics):
    return pltpu.CompilerParams(dimension_semantics=semantics, vmem_limit_bytes=V7X_VMEM_LIMIT_BYTES)


def _layer_norm(t, w, b):
    mu = jnp.mean(t, axis=-1, keepdims=True)
    d = t - mu
    var = jnp.mean(d * d, axis=-1, keepdims=True)
    return d * lax.rsqrt(var + LN_EPS) * w + b


def _log_sigmoid(x):
    return jnp.minimum(x, 0.0) - jnp.log1p(jnp.exp(-jnp.abs(x)))


def _in_proj_kernel(x_ref, cos_ref, sin_ref, wrq, wrk, wrv, wrg, wfq, wfk, wfv, wff, wga, wgb, bf_ref,
                    rq_o, rk_o, rv_o, rg_o, fq_o, fk_o, fv_o, fkb_o, fvt_o, lf_o, ga_o, gb_o):
    xb = x_ref[...].astype(BF16)
    cos = cos_ref[...]
    sin = sin_ref[...]

    def rotary_to(o_ref, y, scale):
        for h in range(RET_HEADS):
            yh = y[:, h * RET_DK:(h + 1) * RET_DK]
            r = yh * cos + pltpu.roll(yh, RET_DK // 2, 1) * sin
            if scale is not None:
                r = r * scale
            o_ref[:, h * RET_DK:(h + 1) * RET_DK] = r

    rotary_to(rq_o, _dot(xb, wrq[...]), None)
    rotary_to(rk_o, _dot(xb, wrk[...]), RET_DK ** -0.5)
    rv_o[...] = _dot(xb, wrv[...]).astype(BF16)
    rg_o[...] = _dot(xb, wrg[...])
    fq_o[...] = (_dot(xb, wfq[...]) * FOX_Q_SCALE).astype(BF16)
    fk = _dot(xb, wfk[...])
    fk_o[...] = fk
    fkb_o[...] = fk.astype(BF16)
    fv = _dot(xb, wfv[...])
    fv_o[...] = fv
    fvt_o[...] = fv.T.astype(BF16)
    ff = _dot(xb, wff[...])[:, :FOX_HEADS] + bf_ref[...]
    lf_o[...] = _log_sigmoid(ff)
    ga_o[...] = _dot(xb, wga[...])
    gb_o[...] = _dot(xb, wgb[...])


def _in_proj(x, cos_t, sin_t, ws, b_f, tm):
    n = x.shape[0]
    n_tab = cos_t.shape[0] // tm
    row = lambda w: pl.BlockSpec((tm, w), lambda i: (i, 0))
    tab = pl.BlockSpec((tm, RET_DK), lambda i: (i % n_tab, 0))
    out_widths = (RET_QK_W, RET_QK_W, RET_V_W, RET_V_W, FOX_W, FOX_W, FOX_W, FOX_W, None, FOX_HEADS,
                  D_MODEL, D_MODEL)
    out_dtypes = (F32, F32, BF16, F32, BF16, F32, F32, BF16, BF16, F32, F32, F32)
    col = pl.BlockSpec((FOX_W, tm), lambda i: (0, i))
    return pl.pallas_call(
        _in_proj_kernel,
        grid=(n // tm,),
        in_specs=[row(D_MODEL), tab, tab] + [_const_spec(w.shape) for w in ws] + [_const_spec(b_f.shape)],
        out_specs=[col if w is None else row(w) for w in out_widths],
        out_shape=[jax.ShapeDtypeStruct((FOX_W, n) if w is None else (n, w), d)
                   for w, d in zip(out_widths, out_dtypes)],
        compiler_params=_params("arbitrary"),
        name="in_proj",
    )(x, cos_t, sin_t, *ws, b_f)


def _prefix_lanes(x):
    n = x.shape[-1]
    lane = lax.broadcasted_iota(jnp.int32, x.shape, x.ndim - 1)
    s = 1
    while s < n:
        x = x + jnp.where(lane >= s, pltpu.roll(x, s, x.ndim - 1), 0.0)
        s *= 2
    return x


def _cumsum_kernel(x_ref, o_ref):
    o_ref[...] = _prefix_lanes(x_ref[...]) * LOG2E


def _cumsum_rows(x):
    return pl.pallas_call(
        _cumsum_kernel,
        out_shape=jax.ShapeDtypeStruct(x.shape, x.dtype),
        name="logf_cumsum",
    )(x)


def _group_norm_gate(o, g, gnw, gnb):
    mu = jnp.mean(o, axis=-1, keepdims=True)
    d = o - mu
    var = jnp.mean(d * d, axis=-1, keepdims=True)
    rn = d * lax.rsqrt(var + GN_EPS) * gnw + gnb
    return (g * jax.nn.sigmoid(g)) * rn


def _retention_kernel(rq_ref, rk_ref, rv_ref, rg_ref, intra_ref, qd_ref, kd_ref, cd_ref, gnw_ref, gnb_ref,
                      a_ref, s_ref):
    j = pl.program_id(1)

    @pl.when(j == 0)
    def _():
        s_ref[...] = jnp.zeros_like(s_ref)

    for h in range(RET_HEADS):
        qs = slice(h * RET_DK, (h + 1) * RET_DK)
        vs = slice(h * RET_DV, (h + 1) * RET_DV)
        q = rq_ref[:, qs]
        k = rk_ref[:, qs]
        v = rv_ref[:, vs]
        s = s_ref[0, h]
        inner = _dot_nt(q.astype(BF16), k.astype(BF16)) * intra_ref[h]
        o = _dot(inner.astype(BF16), v) + _dot((q * qd_ref[h]).astype(BF16), s.astype(BF16))
        kt = (k * kd_ref[h]).T
        s_ref[0, h] = s * cd_ref[h] + _dot(kt.astype(BF16), v)
        a_ref[:, vs] = _group_norm_gate(o, rg_ref[:, vs], gnw_ref[:, vs], gnb_ref[:, vs]).astype(BF16)


def _retention_prompt(rq, rk, rv, rg, tabs, gnw, gnb, batch, seq):
    intra, qd, kd, cd = tabs
    c = intra.shape[-1]
    nc = seq // c
    row = lambda w: pl.BlockSpec((c, w), lambda b, j: (b * nc + j, 0))
    return pl.pallas_call(
        _retention_kernel,
        grid=(batch, nc),
        in_specs=[row(RET_QK_W), row(RET_QK_W), row(RET_V_W), row(RET_V_W),
                  _const_spec(intra.shape), _const_spec(qd.shape), _const_spec(kd.shape), _const_spec(cd.shape),
                  _const_spec(gnw.shape), _const_spec(gnb.shape)],
        out_specs=[row(RET_V_W),
                   pl.BlockSpec((1, RET_HEADS, RET_DK, RET_DV), lambda b, j: (b, 0, 0, 0))],
        out_shape=[jax.ShapeDtypeStruct((batch * seq, RET_V_W), BF16),
                   jax.ShapeDtypeStruct((batch, RET_HEADS, RET_DK, RET_DV), F32)],
        compiler_params=_params("arbitrary", "arbitrary"),
        name="retention_prompt",
    )(rq, rk, rv, rg, intra, qd, kd, cd, gnw, gnb)


def _attn_kernel(q_ref, k_ref, vt_ref, crow_ref, ccol_ref, o_ref, *, tq, tk):
    i = pl.program_id(2)
    q = q_ref[...].astype(F32)
    lane = lax.broadcasted_iota(jnp.int32, (1, LANES), 1)
    key_id = lax.broadcasted_iota(jnp.int32, (tk, tq), 0)
    qry_id = lax.broadcasted_iota(jnp.int32, (tk, tq), 1)
    q0 = pl.multiple_of(i * tq, tq)
    qh = [jnp.where(lane < FOX_HD, q, 0.0).astype(BF16), jnp.where(lane >= FOX_HD, q, 0.0).astype(BF16)]
    ct = [crow_ref[0, 0, hh:hh + 1, pl.ds(q0, tq)] for hh in range(2)]

    def step(kt, carry, diagonal):
        k0 = pl.multiple_of(kt * tk, tk)
        kb = k_ref[pl.ds(k0, tk), :]
        vtb = vt_ref[:, pl.ds(k0, tk)]
        new = []
        for hh in range(2):
            m, l, acc = carry[hh]
            z = _dot_nt(kb, qh[hh]) - ccol_ref[0, 0, pl.ds(k0, tk), hh:hh + 1]
            if diagonal:
                z = jnp.where(key_id <= qry_id, z, NEG_BIG)
            m_new = jnp.maximum(m, jnp.max(z, axis=0, keepdims=True) + ct[hh])
            alpha = jnp.exp2(m - m_new)
            p = jnp.exp2(z + (ct[hh] - m_new))
            l = alpha * l + jnp.sum(p, axis=0, keepdims=True)
            acc = alpha * acc + _dot(vtb, p.astype(BF16))
            new.append((m_new, l, acc))
        return tuple(new)

    init = (jnp.full((1, tq), NEG_BIG, F32), jnp.zeros((1, tq), F32), jnp.zeros((LANES, tq), F32))
    carry = lax.fori_loop(0, i, functools.partial(step, diagonal=False), (init, init))
    (_, l0, acc0), (_, l1, acc1) = step(i, carry, True)
    d_id = lax.broadcasted_iota(jnp.int32, (LANES, 1), 0)
    o_t = jnp.where(d_id < FOX_HD, acc0 / l0, acc1 / l1)
    o_ref[...] = o_t.T.astype(BF16)


def _attention_prompt(fq, fkb, fvt, c_row, c_col, batch, seq):
    tq, tk = ATTN_TQ, ATTN_TK
    assert tq == tk and seq % tq == 0
    nq = seq // tq
    pairs = FOX_HEADS // 2
    return pl.pallas_call(
        functools.partial(_attn_kernel, tq=tq, tk=tk),
        grid=(batch, pairs, nq),
        in_specs=[pl.BlockSpec((tq, LANES), lambda b, p, i: (b * nq + i, p)),
                  pl.BlockSpec((seq, LANES), lambda b, p, i: (b, p)),
                  pl.BlockSpec((LANES, seq), lambda b, p, i: (p, b)),
                  pl.BlockSpec((1, 1, 2, seq), lambda b, p, i: (b, p, 0, 0)),
                  pl.BlockSpec((1, 1, seq, 2), lambda b, p, i: (b, p, 0, 0))],
        out_specs=pl.BlockSpec((tq, LANES), lambda b, p, i: (b * nq + i, p)),
        out_shape=jax.ShapeDtypeStruct((batch * seq, FOX_W), BF16),
        compiler_params=_params("arbitrary", "arbitrary", "arbitrary"),
        name="fox_attention_prompt",
    )(fq, fkb, fvt, c_row, c_col)


def _merge_kernel(a_ref, fo_ref, ga_ref, gb_ref, x_ref, wpa, wpb, wo, lnw, lnb, h_ref):
    ua = _dot(a_ref[...], wpa[...])
    ub = _dot(fo_ref[...], wpb[...])
    merged = jax.nn.sigmoid(ga_ref[...]) * ua + jax.nn.sigmoid(gb_ref[...]) * ub
    mix = _dot(merged.astype(BF16), wo[...])
    h_ref[...] = _layer_norm(ALPHA * x_ref[...] + mix, lnw[...], lnb[...])


def _merge(a, fo, ga, gb, x, wpa, wpb, wo, lnw, lnb, tm):
    n = x.shape[0]
    row = lambda w: pl.BlockSpec((tm, w), lambda i: (i, 0))
    return pl.pallas_call(
        _merge_kernel,
        grid=(n // tm,),
        in_specs=[row(RET_V_W), row(FOX_W), row(D_MODEL), row(D_MODEL), row(D_MODEL)]
                 + [_const_spec(w.shape) for w in (wpa, wpb, wo, lnw, lnb)],
        out_specs=row(D_MODEL),
        out_shape=jax.ShapeDtypeStruct((n, D_MODEL), F32),
        compiler_params=_params("arbitrary"),
        name="merge_ln1",
    )(a, fo, ga, gb, x, wpa, wpb, wo, lnw, lnb)


def _ffn_tail(h, u, u1, u2, g, cw, cb, wdown, lnw, lnb):
    acc = cb + u2 * cw[0:1] + u1 * cw[1:2] + u * cw[2:3]
    a = jax.nn.gelu(acc) * g
    f = _dot(a.astype(BF16), wdown)
    return _layer_norm(ALPHA * h + f, lnw, lnb)


def _ffn_prompt_kernel(h_ref, wup, wgate, cw_ref, cb_ref, wdown, lnw, lnb, y_ref, conv_ref, prev_ref, *, tm, tps):
    i = pl.program_id(0)
    h = h_ref[...]
    hb = h.astype(BF16)
    u = _dot(hb, wup[...])
    g = _dot(hb, wgate[...])
    @pl.when(i % tps == 0)
    def _():
        prev_ref[...] = jnp.zeros_like(prev_ref)

    prev = prev_ref[...]
    rid = lax.broadcasted_iota(jnp.int32, (tm, 1), 0)
    u1 = jnp.where(rid == 0, prev[7:8], pltpu.roll(u, 1, 0))
    u2 = jnp.where(rid == 0, prev[6:7], jnp.where(rid == 1, prev[7:8], pltpu.roll(u, 2, 0)))
    prev_ref[...] = u[tm - 8:tm]
    conv_ref[0] = u[tm - (CONV_W - 1):tm]
    y_ref[...] = _ffn_tail(h, u, u1, u2, g, cw_ref[...], cb_ref[...], wdown[...], lnw[...], lnb[...])


def _ffn_prompt(h, wup, wgate, cw, cb, wdown, lnw, lnb, batch, seq, tm):
    n = h.shape[0]
    tps = seq // tm
    row = pl.BlockSpec((tm, D_MODEL), lambda i: (i, 0))
    return pl.pallas_call(
        functools.partial(_ffn_prompt_kernel, tm=tm, tps=tps),
        grid=(n // tm,),
        in_specs=[row] + [_const_spec(w.shape) for w in (wup, wgate, cw, cb, wdown, lnw, lnb)],
        out_specs=[row, pl.BlockSpec((1, CONV_W - 1, D_FF), lambda i: (i // tps, 0, 0))],
        out_shape=[jax.ShapeDtypeStruct((n, D_MODEL), F32),
                   jax.ShapeDtypeStruct((batch, CONV_W - 1, D_FF), F32)],
        scratch_shapes=[pltpu.VMEM((8, D_FF), F32)],
        compiler_params=_params("arbitrary"),
        name="conv_ffn_prompt",
    )(h, wup, wgate, cw, cb, wdown, lnw, lnb)


def _ffn_sample_kernel(h_ref, sc0_ref, sc1_ref, wup, wgate, cw_ref, cb_ref, wdown, lnw, lnb, y_ref, u_ref):
    h = h_ref[...]
    hb = h.astype(BF16)
    u = _dot(hb, wup[...])
    g = _dot(hb, wgate[...])
    u_ref[...] = u
    y_ref[...] = _ffn_tail(h, u, sc1_ref[...], sc0_ref[...], g, cw_ref[...], cb_ref[...], wdown[...],
                           lnw[...], lnb[...])


def _ffn_sample(h, sc0, sc1, wup, wgate, cw, cb, wdown, lnw, lnb):
    n = h.shape[0]
    return pl.pallas_call(
        _ffn_sample_kernel,
        out_shape=[jax.ShapeDtypeStruct((n, D_MODEL), F32), jax.ShapeDtypeStruct((n, D_FF), F32)],
        compiler_params=pltpu.CompilerParams(vmem_limit_bytes=V7X_VMEM_LIMIT_BYTES),
        name="conv_ffn_sample",
    )(h, sc0, sc1, wup, wgate, cw, cb, wdown, lnw, lnb)


def _row_to_col(row, eye):
    n = row.shape[-1]
    return jnp.sum(jnp.where(eye, jnp.broadcast_to(row, (n, n)), 0.0), axis=1, keepdims=True)


def _retention_sample_kernel(rq_ref, rk_ref, rv_ref, rg_ref, s_ref, intra_ref, qd_ref, kd_ref, cd_ref,
                             gnw_ref, gnb_ref, a_ref, so_ref):
    eye = (lax.broadcasted_iota(jnp.int32, (RET_DK, RET_DK), 0)
           == lax.broadcasted_iota(jnp.int32, (RET_DK, RET_DK), 1))
    for h in range(RET_HEADS):
        qs = slice(h * RET_DK, (h + 1) * RET_DK)
        vs = slice(h * RET_DV, (h + 1) * RET_DV)
        q = rq_ref[0, :, qs]
        k = rk_ref[0, :, qs]
        v = rv_ref[0, :, vs].astype(F32)
        s = s_ref[0, h]
        inner = jnp.sum(q * k, axis=1, keepdims=True) * intra_ref[h]
        qcol = _row_to_col(q * qd_ref[h], eye)
        kcol = _row_to_col(k * kd_ref[h], eye)
        o = inner * v + jnp.sum(qcol * s, axis=0, keepdims=True)
        so_ref[0, h] = s * cd_ref[h] + kcol * v
        a_ref[0, :, vs] = _group_norm_gate(o, rg_ref[0, :, vs], gnw_ref[:, vs], gnb_ref[:, vs]).astype(BF16)


def _retention_sample(rq, rk, rv, rg, state, tabs, gnw, gnb):
    intra, qd, kd, cd = tabs
    nb = rq.shape[0]
    row = lambda w: pl.BlockSpec((1, 1, w), lambda b: (b, 0, 0))
    st = pl.BlockSpec((1, RET_HEADS, RET_DK, RET_DV), lambda b: (b, 0, 0, 0))
    r3 = lambda t: t.reshape(nb, 1, t.shape[-1])
    a, s_new = pl.pallas_call(
        _retention_sample_kernel,
        grid=(nb,),
        in_specs=[row(RET_QK_W), row(RET_QK_W), row(RET_V_W), row(RET_V_W), st]
                 + [_const_spec(t.shape) for t in (intra, qd, kd, cd, gnw, gnb)],
        out_specs=[row(RET_V_W), st],
        out_shape=[jax.ShapeDtypeStruct((nb, 1, RET_V_W), BF16),
                   jax.ShapeDtypeStruct(state.shape, F32)],
        compiler_params=_params("arbitrary"),
        name="retention_sample",
    )(r3(rq), r3(rk), r3(rv), r3(rg), state, intra, qd, kd, cd, gnw, gnb)
    return a.reshape(nb, RET_V_W), s_new


def _decode_attn_kernel(pt_ref, q_ref, kn_ref, vn_ref, lfn_ref, *refs, g_pages):
    del pt_ref
    k_refs = refs[:g_pages]
    v_refs = refs[g_pages:2 * g_pages]
    lf_refs = refs[2 * g_pages:3 * g_pages]
    o_ref, m_ref, l_ref, c_ref, acc_ref = refs[3 * g_pages:]
    j = pl.program_id(1)

    @pl.when(j == 0)
    def _():
        m_ref[...] = jnp.full_like(m_ref, NEG_BIG)
        l_ref[...] = jnp.zeros_like(l_ref)
        c_ref[...] = jnp.zeros_like(c_ref)
        acc_ref[...] = jnp.zeros_like(acc_ref)

    head_of_lane = lax.broadcasted_iota(jnp.int32, (FOX_HEADS, FOX_W), 1) // FOX_HD
    own = head_of_lane == lax.broadcasted_iota(jnp.int32, (FOX_HEADS, FOX_W), 0)
    qbd = jnp.where(own, jnp.broadcast_to(q_ref[0].astype(F32), (FOX_HEADS, FOX_W)), 0.0).astype(BF16)

    def update(zs, pv_fn):
        m = m_ref[...]
        m_new = m
        for z in zs:
            m_new = jnp.maximum(m_new, jnp.max(z, axis=1, keepdims=True))
        alpha = jnp.exp2(m - m_new)
        ps = [jnp.exp2(z - m_new) for z in zs]
        l = alpha * l_ref[...]
        for p in ps:
            l = l + jnp.sum(p, axis=1, keepdims=True)
        l_ref[...] = l
        acc_ref[...] = alpha * acc_ref[...] + pv_fn(ps)
        m_ref[...] = m_new

    carry = c_ref[...]
    zs = []
    for g in range(g_pages):
        c = carry + _prefix_lanes(lf_refs[g][0]) * LOG2E
        carry = c[:, PAGE_SIZE - 1:PAGE_SIZE]
        zs.append(_dot(qbd, k_refs[g][0].astype(BF16)) - c)
    c_ref[...] = carry

    def pv_pages(ps):
        pv = _dot_nt(ps[0].astype(BF16), v_refs[0][0].astype(BF16))
        for g in range(1, g_pages):
            pv = pv + _dot_nt(ps[g].astype(BF16), v_refs[g][0].astype(BF16))
        return pv

    update(zs, pv_pages)

    @pl.when(j == pl.num_programs(1) - 1)
    def _():
        kn = kn_ref[0].astype(BF16).astype(F32)
        vn = vn_ref[0].astype(BF16).astype(F32)
        z_self = jnp.sum(qbd.astype(F32) * kn, axis=1, keepdims=True) - (c_ref[...] + lfn_ref[0] * LOG2E)
        update([z_self], lambda ps: ps[0].astype(BF16).astype(F32) * vn)
        o = jnp.where(own, acc_ref[...] / l_ref[...], 0.0)
        o_ref[0] = jnp.sum(o, axis=0, keepdims=True).astype(BF16)


def _attention_sample(page_table, fq, k_new, v_new, lf_new, cache_kt, cache_vt, cache_lft):
    nb, n_pages = page_table.shape
    gp = PAGES_PER_STEP
    assert n_pages % gp == 0
    row = lambda w: pl.BlockSpec((1, 1, w), lambda b, j, pt: (b, 0, 0))

    def page(g, rows):
        return pl.BlockSpec((1, rows, PAGE_SIZE), lambda b, j, pt: (pt[b, j * gp + g], 0, 0))

    in_specs = ([row(FOX_W), row(FOX_W), row(FOX_W), pl.BlockSpec((1, FOX_HEADS, 1), lambda b, j, pt: (b, 0, 0))]
                + [page(g, FOX_W) for g in range(gp)]
                + [page(g, FOX_W) for g in range(gp)]
                + [page(g, FOX_HEADS) for g in range(gp)])
    r3 = lambda t: t.reshape(nb, 1, t.shape[-1])
    out = pl.pallas_call(
        functools.partial(_decode_attn_kernel, g_pages=gp),
        grid_spec=pltpu.PrefetchScalarGridSpec(
            num_scalar_prefetch=1,
            grid=(nb, n_pages // gp),
            in_specs=in_specs,
            out_specs=row(FOX_W),
            scratch_shapes=[pltpu.VMEM((FOX_HEADS, 1), F32), pltpu.VMEM((FOX_HEADS, 1), F32),
                            pltpu.VMEM((FOX_HEADS, 1), F32), pltpu.VMEM((FOX_HEADS, FOX_W), F32)]),
        out_shape=jax.ShapeDtypeStruct((nb, 1, FOX_W), BF16),
        compiler_params=_params("arbitrary", "arbitrary"),
        name="fox_attention_sample",
    )(page_table, r3(fq), r3(k_new), r3(v_new), lf_new.reshape(nb, FOX_HEADS, 1),
      *([cache_kt] * gp), *([cache_vt] * gp), *([cache_lft] * gp))
    return out.reshape(nb, FOX_W)


def _rotary_tables(pos):
    half = RET_DK // 2
    freq = ROPE_BASE ** (-jnp.arange(half, dtype=F32) / half)
    ang = pos[:, None] * freq[None, :]
    cos, sin = jnp.cos(ang), jnp.sin(ang)
    return jnp.concatenate([cos, cos], -1), jnp.concatenate([-sin, sin], -1)


def _decay_tables(c):
    log_g = jnp.log(1.0 - 2.0 ** (-5.0 - jnp.arange(RET_HEADS, dtype=F32)))
    i = jnp.arange(c, dtype=F32)
    rel = i[:, None] - i[None, :]
    intra = jnp.where(rel >= 0, jnp.exp(log_g[:, None, None] * jnp.maximum(rel, 0.0)), 0.0)
    q_decay = jnp.exp(log_g[:, None] * (i[None, :] + 1.0))
    k_decay = jnp.exp(log_g[:, None] * (c - 1.0 - i[None, :]))
    chunk_decay = jnp.exp(log_g * c)
    bc = lambda t, w: jnp.broadcast_to(t[:, :, None], (RET_HEADS, c, w))
    return intra, bc(q_decay, RET_DK), bc(k_decay, RET_DK), jnp.broadcast_to(
        chunk_decay[:, None, None], (RET_HEADS, 1, RET_DV))


def _split_w_in(w_in):
    offs = [0]
    for n in IN_SPLITS:
        offs.append(offs[-1] + n)
    parts = [w_in[:, offs[i]:offs[i + 1]].astype(BF16) for i in range(len(IN_SPLITS))]
    parts[7] = jnp.pad(parts[7], ((0, 0), (0, LANES - FOX_HEADS)))
    return parts


def kernel(x_prompt, x_sample, cache_k, cache_v, cache_logf, state_ret, state_conv, page_table,
           w_in, b_f, ret_gn_w, ret_gn_b, w_pa, w_pb, w_o, ln1_w, ln1_b,
           w_up, w_gate, conv_w, conv_b, w_down, ln2_w, ln2_b):
    assert w_in.shape[0] == DEPTH
    bp, tp, _ = x_prompt.shape
    bs, ts, _ = x_sample.shape
    assert ts == 1
    n_pool = cache_k.shape[1]
    past = page_table.shape[1] * PAGE_SIZE
    np_tok = bp * tp

    ws = _split_w_in(w_in[0])
    bf2 = b_f[0].reshape(1, FOX_HEADS)
    gnw, gnb = ret_gn_w[0].reshape(1, RET_V_W), ret_gn_b[0].reshape(1, RET_V_W)
    wpa, wpb, wo = w_pa[0].astype(BF16), w_pb[0].astype(BF16), w_o[0].astype(BF16)
    wup, wgate, wdown = w_up[0].astype(BF16), w_gate[0].astype(BF16), w_down[0].astype(BF16)
    ln1w, ln1b = ln1_w[0].reshape(1, D_MODEL), ln1_b[0].reshape(1, D_MODEL)
    ln2w, ln2b = ln2_w[0].reshape(1, D_MODEL), ln2_b[0].reshape(1, D_MODEL)
    cw, cb = conv_w[0], conv_b[0].reshape(1, D_FF)

    xp = x_prompt.reshape(np_tok, D_MODEL)
    cos_p, sin_p = _rotary_tables(jnp.arange(tp, dtype=F32))
    rq, rk, rv, rg, fq, fk, fv, fkb, fvt, lf, ga, gb = _in_proj(xp, cos_p, sin_p, ws, bf2, IN_PROJ_TM)
    a_p, ret_p = _retention_prompt(rq, rk, rv, rg, _decay_tables(CHUNK if tp % CHUNK == 0 else tp), gnw, gnb, bp, tp)
    lf_t = lf.reshape(bp, tp, FOX_HEADS).transpose(0, 2, 1)
    c_row = _cumsum_rows(lf_t.reshape(bp * FOX_HEADS, tp)).reshape(bp, FOX_HEADS // 2, 2, tp)
    c_col = c_row.transpose(0, 1, 3, 2)
    fo_p = _attention_prompt(fq, fkb, fvt, c_row, c_col, bp, tp)
    h_p = _merge(a_p, fo_p, ga, gb, xp, wpa, wpb, wo, ln1w, ln1b, MERGE_TM)
    y_p, conv_p = _ffn_prompt(h_p, wup, wgate, cw, cb, wdown, ln2w, ln2b, bp, tp, FFN_TM)

    xs = x_sample.reshape(bs, D_MODEL)
    cos_s, sin_s = _rotary_tables(jnp.full((bs,), past, dtype=F32))
    rq, rk, rv, rg, fq, fk_s, fv_s, _, _, lf_s, ga, gb = _in_proj(xs, cos_s, sin_s, ws, bf2, bs)
    a_s, ret_s = _retention_sample(rq, rk, rv, rg, state_ret[0], _decay_tables(1), gnw, gnb)
    cache_kt = cache_k[0].transpose(0, 2, 3, 1).reshape(n_pool, FOX_W, PAGE_SIZE)
    cache_vt = cache_v[0].transpose(0, 2, 3, 1).reshape(n_pool, FOX_W, PAGE_SIZE)
    cache_lft = cache_logf[0].transpose(0, 2, 1)
    fo_s = _attention_sample(page_table, fq, fk_s, fv_s, lf_s, cache_kt, cache_vt, cache_lft)
    h_s = _merge(a_s, fo_s, ga, gb, xs, wpa, wpb, wo, ln1w, ln1b, bs)
    sc0, sc1 = state_conv[0, :, 0, :], state_conv[0, :, 1, :]
    y_s, u_s = _ffn_sample(h_s, sc0, sc1, wup, wgate, cw, cb, wdown, ln2w, ln2b)

    return (y_p.reshape(bp, tp, D_MODEL), y_s.reshape(bs, ts, D_MODEL),
            fk.reshape(DEPTH, bp, tp, FOX_HEADS, FOX_HD), fv.reshape(DEPTH, bp, tp, FOX_HEADS, FOX_HD),
            lf.reshape(DEPTH, bp, tp, FOX_HEADS),
            fk_s.reshape(DEPTH, bs, ts, FOX_HEADS, FOX_HD), fv_s.reshape(DEPTH, bs, ts, FOX_HEADS, FOX_HD),
            lf_s.reshape(DEPTH, bs, ts, FOX_HEADS),
            ret_p[None], ret_s[None], conv_p[None], jnp.stack([sc1, u_s], axis=1)[None])
```
